```python
import math
import jax
import jax.numpy as jnp
from jax import lax
import numpy as np

D_MODEL = 2048
BATCH = 2
SEQ = 4096
DEPTH = 4
DEC_BATCH = 8
DEC_SEQ = 1
PAST_LEN = 16384
PAGE_SIZE = 128

HD = 64
N_BRANCH = 4
H_MIX = D_MODEL // (N_BRANCH * HD)
BRANCH_W = H_MIX * HD
KV_DSA = 2
G_DSA = H_MIX // KV_DSA
H_IDX = 8
D_IDX = 64
TOPK_DSA = 256
KV_NSA = 2
G_NSA = H_MIX // KV_NSA
CMP_LEN = 32
CMP_STRIDE = 16
SEL_LEN = 64
SEL_TOP = 16
WINDOW = 512
N_BUCKETS = 32
MAX_DIST = 128
QBLK = 128
PLE_DIM = 256
D_FF = -(-8 * D_MODEL // (3 * 256)) * 256
ALPHA = (2 * DEPTH) ** 0.25
BETA_INIT = (8 * DEPTH) ** -0.25
LN_EPS = 1e-5
NEG = -1e30
BIG = 1e30

KV_WIDTHS = {'sb_k': H_MIX * HD, 'sb_v': H_MIX * HD, 'fox_k': H_MIX * HD, 'fox_v': H_MIX * HD,
             'dsa_k': KV_DSA * HD, 'dsa_v': KV_DSA * HD, 'dsa_kidx': D_IDX,
             'nsa_kc': KV_NSA * HD, 'nsa_vc': KV_NSA * HD, 'nsa_ks': KV_NSA * HD, 'nsa_vs': KV_NSA * HD}
WIN_WIDTHS = {'nsa_kw': KV_NSA * HD, 'nsa_vw': KV_NSA * HD}
Q_WIDTHS = {'sb_q': H_MIX * HD, 'fox_q': H_MIX * HD, 'dsa_q': H_MIX * HD, 'idx_q': H_IDX * D_IDX,
            'idx_w': H_IDX, 'nsa_q': H_MIX * HD, 'nsa_g': 3 * H_MIX, 'branch_g': N_BRANCH * D_MODEL}
C_KV = sum(KV_WIDTHS.values())
C_WIN = sum(WIN_WIDTHS.values())
C_Q = sum(Q_WIDTHS.values())
C_PROJ = C_KV + H_MIX + C_WIN + C_Q

kernel_name = 'hybrid_sb_fox_dsa_nsa_decode_step'


def _split(x, widths):
    out, off = {}, 0
    for name, w in widths.items():
        out[name] = x[..., off:off + w]
        off += w
    return out


def _heads(a, n):
    return a.reshape(a.shape[:-1] + (n, HD))


def layer_norm(x, g, b):
    xf = x.astype(jnp.float32)
    mu = jnp.mean(xf, axis=-1, keepdims=True)
    var = jnp.mean(jnp.square(xf - mu), axis=-1, keepdims=True)
    return ((xf - mu) * lax.rsqrt(var + LN_EPS) * g + b).astype(x.dtype)


def t5_bucket(rel):
    n = jnp.maximum(rel, 0)
    max_exact = N_BUCKETS // 2
    nf = jnp.maximum(n, 1).astype(jnp.float32)
    large = max_exact + (jnp.log(nf / max_exact) / math.log(MAX_DIST / max_exact)
                         * (N_BUCKETS - max_exact)).astype(jnp.int32)
    return jnp.where(n < max_exact, n, jnp.minimum(large, N_BUCKETS - 1))


def t5_bias(table, rel):
    return table[t5_bucket(rel)].astype(jnp.float32)


def masked_softmax(logits, mask):
    p = jax.nn.softmax(jnp.where(mask, logits, NEG), axis=-1)
    return jnp.where(mask, p, 0.0)


def nsa_compress(k, w1, w2, pos_emb):
    L = k.shape[1]
    n_cmp = (L - CMP_LEN) // CMP_STRIDE + 1
    idx = jnp.arange(n_cmp)[:, None] * CMP_STRIDE + jnp.arange(CMP_LEN)[None, :]
    blocks = k[:, idx] + pos_emb[None, None, :, None, :]
    hid = jax.nn.gelu(jnp.einsum('bnlgd,lde->bnge', blocks, w1.reshape(CMP_LEN, HD, HD)))
    return jnp.einsum('bnge,ef->bngf', hid, w2)


def token_mixers(kv, logf, win, qs, lp, rpb_table):
    B, L = logf.shape[:2]
    Tq = qs['sb_q'].shape[1]
    q0 = L - Tq
    qb = QBLK if Tq % QBLK == 0 else Tq
    n_blk = Tq // qb
    dt = qs['sb_q'].dtype
    scale = HD ** -0.5
    kpos = jnp.arange(L)

    k_sb, v_sb = _heads(kv['sb_k'], H_MIX), _heads(kv['sb_v'], H_MIX)
    k_fx, v_fx = _heads(kv['fox_k'], H_MIX), _heads(kv['fox_v'], H_MIX)
    cum_f = jnp.cumsum(logf.astype(jnp.float32), axis=1).transpose(0, 2, 1)
    k_ds, v_ds = _heads(kv['dsa_k'], KV_DSA), _heads(kv['dsa_v'], KV_DSA)
    k_ix = kv['dsa_kidx']
    k_top = min(TOPK_DSA, L // 4)

    kc = nsa_compress(_heads(kv['nsa_kc'], KV_NSA), lp['w_cmp1'][0], lp['w_cmp2'][0], lp['cmp_pos'])
    vc = nsa_compress(_heads(kv['nsa_vc'], KV_NSA), lp['w_cmp1'][1], lp['w_cmp2'][1], lp['cmp_pos'])
    n_cmp = kc.shape[1]
    e_cmp = jnp.arange(n_cmp) * CMP_STRIDE + (CMP_LEN - 1)
    n_sel = -(-L // SEL_LEN)
    n_top = min(SEL_TOP, n_sel)

    def to_blocks(a):
        a = jnp.pad(_heads(a, KV_NSA), ((0, 0), (0, n_sel * SEL_LEN - L), (0, 0), (0, 0)))
        return a.reshape(B, n_sel, SEL_LEN, KV_NSA, HD).transpose(0, 3, 1, 2, 4)

    ks_blk, vs_blk = to_blocks(kv['nsa_ks']), to_blocks(kv['nsa_vs'])
    kw, vw = _heads(win['nsa_kw'], KV_NSA), _heads(win['nsa_vw'], KV_NSA)
    ratio = SEL_LEN // CMP_STRIDE
    padl = CMP_LEN // CMP_STRIDE - 1
    tab_ds = rpb_table[:, :H_MIX]
    tab_ns = rpb_table[:, H_MIX:]
    tab_ns_g = tab_ns.reshape(N_BUCKETS, KV_NSA, G_NSA).transpose(1, 0, 2)
    gather_rows = jax.vmap(lambda a, i: a[i])
    gather_blocks = jax.vmap(gather_rows)

    def block(bi):
        i0 = bi * qb
        t = q0 + i0 + jnp.arange(qb)
        qsl = lambda a: lax.dynamic_slice_in_dim(a, i0, qb, axis=1)

        q = _heads(qsl(qs['sb_q']), H_MIX)
        z = jnp.einsum('bqhd,bkhd->bhqk', q, k_sb).astype(jnp.float32) * scale
        m = kpos[None, :] < t[:, None]
        u = jnp.where(m, jax.nn.log_sigmoid(-z), 0.0)
        log_w = jax.nn.log_sigmoid(z) + lax.cumsum(u, axis=3, reverse=True) - u
        a = jnp.where(m, jnp.exp(log_w), 0.0)
        o_sb = jnp.einsum('bhqk,bkhd->bqhd', a.astype(dt), v_sb)

        q = _heads(qsl(qs['fox_q']), H_MIX)
        c_t = lax.dynamic_slice_in_dim(cum_f, q0 + i0, qb, axis=2)
        s = jnp.einsum('bqhd,bkhd->bhqk', q, k_fx).astype(jnp.float32) * scale
        s = s + c_t[..., None] - cum_f[:, :, None, :]
        p = masked_softmax(s, kpos[None, :] <= t[:, None])
        o_fx = jnp.einsum('bhqk,bkhd->bqhd', p.astype(dt), v_fx)

        iq = qsl(qs['idx_q']).reshape(B, qb, H_IDX, D_IDX)
        iw = qsl(qs['idx_w']).astype(jnp.float32)
        isc = jax.nn.relu(jnp.einsum('bqhd,bkd->bqhk', iq, k_ix).astype(jnp.float32))
        isc = jnp.einsum('bqhk,bqh->bqk', isc, iw)
        isc = jnp.where(kpos[None, None, :] <= t[None, :, None], isc, NEG)
        _, sel = lax.top_k(isc, k_top)
        ok = sel <= t[None, :, None]
        ksel, vsel = gather_rows(k_ds, sel), gather_rows(v_ds, sel)
        q = qsl(qs['dsa_q']).reshape(B, qb, KV_DSA, G_DSA, HD)
        s = jnp.einsum('bqgrd,bqjgd->bqgrj', q, ksel).astype(jnp.float32) * scale
        bias = t5_bias(tab_ds, t[None, :, None] - sel).reshape(B, qb, k_top, KV_DSA, G_DSA)
        s = s + bias.transpose(0, 1, 3, 4, 2)
        p = masked_softmax(s, ok[:, :, None, None, :])
        o_ds = jnp.einsum('bqgrj,bqjgd->bqgrd', p.astype(dt), vsel)

        q = qsl(qs['nsa_q']).reshape(B, qb, KV_NSA, G_NSA, HD)
        s = jnp.einsum('bqgrd,bngd->bqgrn', q, kc).astype(jnp.float32) * scale
        rel_c = t[:, None] - e_cmp[None, :]
        s = s + t5_bias(tab_ns, rel_c).reshape(qb, n_cmp, KV_NSA, G_NSA).transpose(0, 2, 3, 1)[None]
        pc = masked_softmax(s, (rel_c >= 0)[None, :, None, None, :])
        o_c = jnp.einsum('bqgrn,bngd->bqgrd', pc.astype(dt), vc)
        imp = jnp.pad(pc.sum(axis=3), ((0, 0), (0, 0), (0, 0), (padl, (n_sel + 1) * ratio - n_cmp)))
        imp_sel = sum(imp[..., o:o + n_sel * ratio:ratio] for o in range(padl + ratio))
        blk = jnp.arange(n_sel)[None, :]
        tb = (t // SEL_LEN)[:, None]
        forced = (blk == 0) | (blk == tb) | (blk == tb - 1)
        score = jnp.where(forced[None, :, None, :], BIG, jnp.where((blk <= tb)[None, :, None, :], imp_sel, NEG))
        _, bsel = lax.top_k(score, n_top)
        bsel = bsel.transpose(0, 2, 1, 3)
        kg, vg = gather_blocks(ks_blk, bsel), gather_blocks(vs_blk, bsel)
        pos = bsel[..., None] * SEL_LEN + jnp.arange(SEL_LEN)
        rel_s = t[None, None, :, None, None] - pos
        s = jnp.einsum('bqgrd,bgqjsd->bgqrjs', q, kg).astype(jnp.float32) * scale
        bias = jax.vmap(t5_bias, in_axes=(0, 1), out_axes=1)(tab_ns_g, rel_s)
        s = (s + bias.transpose(0, 1, 2, 5, 3, 4)).reshape(B, KV_NSA, qb, G_NSA, n_top * SEL_LEN)
        p = masked_softmax(s, (rel_s >= 0).reshape(B, KV_NSA, qb, 1, n_top * SEL_LEN))
        o_s = jnp.einsum('bgqrm,bgqmd->bqgrd', p.astype(dt), vg.reshape(B, KV_NSA, qb, n_top * SEL_LEN, HD))
        kwb = lax.dynamic_slice_in_dim(kw, i0, WINDOW + qb, axis=1)
        vwb = lax.dynamic_slice_in_dim(vw, i0, WINDOW + qb, axis=1)
        wpos = q0 + i0 - WINDOW + jnp.arange(WINDOW + qb)
        rel_w = t[:, None] - wpos[None, :]
        mw = (wpos[None, :] >= 0) & (rel_w >= 0) & (rel_w <= WINDOW)
        s = jnp.einsum('bqgrd,bkgd->bqgrk', q, kwb).astype(jnp.float32) * scale
        s = s + t5_bias(tab_ns, rel_w).reshape(qb, WINDOW + qb, KV_NSA, G_NSA).transpose(0, 2, 3, 1)[None]
        p = masked_softmax(s, mw[None, :, None, None, :])
        o_w = jnp.einsum('bqgrk,bkgd->bqgrd', p.astype(dt), vwb)
        g = jax.nn.sigmoid(qsl(qs['nsa_g']).astype(jnp.float32)).reshape(B, qb, 3, KV_NSA, G_NSA)[..., None]
        o_ns = g[:, :, 0] * o_c + g[:, :, 1] * o_s + g[:, :, 2] * o_w

        return jnp.stack([o_sb.reshape(B, qb, BRANCH_W).astype(dt), o_fx.reshape(B, qb, BRANCH_W).astype(dt),
                          o_ds.reshape(B, qb, BRANCH_W).astype(dt), o_ns.reshape(B, qb, BRANCH_W).astype(dt)], axis=2)

    out = lax.map(block, jnp.arange(n_blk))
    return out.transpose(1, 0, 2, 3, 4).reshape(B, Tq, N_BRANCH, BRANCH_W)


def trunk_layer(x, p_emb, past_rows, past_logf, win_prefix, lp, rpb_table):
    B, T, _ = x.shape
    h = x @ lp['w_in']
    o1, o2, o3 = C_KV, C_KV + H_MIX, C_KV + H_MIX + C_WIN
    new_rows = h[..., :o1]
    new_logf = jax.nn.log_sigmoid(h[..., o1:o2].astype(jnp.float32) + lp['b_forget']).astype(x.dtype)
    new_win = h[..., o2:o3]
    qs = _split(h[..., o3:], Q_WIDTHS)
    if past_rows is None:
        rows, logf = new_rows, new_logf
    else:
        rows = jnp.concatenate([past_rows, new_rows.astype(past_rows.dtype)], axis=1)
        logf = jnp.concatenate([past_logf, new_logf.astype(past_logf.dtype)], axis=1)
    win_all = jnp.concatenate([win_prefix, new_win.astype(win_prefix.dtype)], axis=1)
    br = token_mixers(_split(rows, KV_WIDTHS), logf, _split(win_all, WIN_WIDTHS), qs, lp, rpb_table)
    up = jnp.einsum('btnc,ncd->btnd', br, lp['w_branch'])
    gate = jax.nn.sigmoid(qs['branch_g']).reshape(B, T, N_BRANCH, D_MODEL)
    mix = jnp.einsum('btnd,btnd->btd', gate, up) @ lp['w_out']
    x = layer_norm(ALPHA * x + mix.astype(x.dtype), lp['ln1_g'], lp['ln1_b'])
    ff = (jax.nn.silu(x @ lp['w_gate']) * (x @ lp['w_up'])) @ lp['w_down']
    x = layer_norm(ALPHA * x + ff, lp['ln2_g'], lp['ln2_b'])
    x = x + jax.nn.sigmoid(x @ lp['w_pg']) * (p_emb @ lp['w_pe'])
    return x, new_rows, new_logf, new_win


def setup_inputs(seed: int = 0) -> dict:
    key = jax.random.key(seed)
    ks = jax.random.split(key, 26)
    n_pages = PAST_LEN // PAGE_SIZE
    n_used = DEC_BATCH * n_pages
    n_pool = n_used + n_used // 4
    win_s = min(WINDOW, PAST_LEN)

    def nrm(k, shape, s=1.0):
        return s * jax.random.normal(k, shape, jnp.float32)

    page_table = jax.random.permutation(ks[5], n_pool)[:n_used].reshape(DEC_BATCH, n_pages).astype(jnp.int32)
    return {
        'x_prompt': nrm(ks[0], (BATCH, SEQ, D_MODEL)),
        'x_sample': nrm(ks[1], (DEC_BATCH, DEC_SEQ, D_MODEL)),
        'cache_kv': nrm(ks[2], (DEPTH, n_pool, PAGE_SIZE, C_KV)),
        'cache_logf': jax.nn.log_sigmoid(1.0 + 4.0 * jax.random.uniform(ks[3], (DEPTH, n_pool, PAGE_SIZE, H_MIX), jnp.float32)),
        'state_win': nrm(ks[4], (DEPTH, DEC_BATCH, win_s, C_WIN)),
        'page_table': page_table,
        'p_prompt': nrm(ks[6], (DEPTH, BATCH, SEQ, PLE_DIM)),
        'p_sample': nrm(ks[7], (DEPTH, DEC_BATCH, DEC_SEQ, PLE_DIM)),
        'w_in': nrm(ks[8], (DEPTH, D_MODEL, C_PROJ), D_MODEL ** -0.5),
        'b_forget': 1.0 + 4.0 * jax.random.uniform(ks[9], (DEPTH, H_MIX), jnp.float32),
        'w_cmp1': nrm(ks[10], (DEPTH, 2, CMP_LEN * HD, HD), (CMP_LEN * HD) ** -0.5),
        'w_cmp2': nrm(ks[11], (DEPTH, 2, HD, HD), HD ** -0.5),
        'cmp_pos': nrm(ks[12], (DEPTH, CMP_LEN, HD), 0.1),
        'w_branch': nrm(ks[13], (DEPTH, N_BRANCH, BRANCH_W, D_MODEL), BRANCH_W ** -0.5),
        'w_out': nrm(ks[14], (DEPTH, D_MODEL, D_MODEL), BETA_INIT * D_MODEL ** -0.5),
        'ln1_g': 1.0 + nrm(ks[15], (DEPTH, D_MODEL), 0.05),
        'ln1_b': nrm(ks[16], (DEPTH, D_MODEL), 0.02),
        'w_gate': nrm(ks[17], (DEPTH, D_MODEL, D_FF), D_MODEL ** -0.5),
        'w_up': nrm(ks[18], (DEPTH, D_MODEL, D_FF), D_MODEL ** -0.5),
        'w_down': nrm(ks[19], (DEPTH, D_FF, D_MODEL), BETA_INIT * D_FF ** -0.5),
        'ln2_g': 1.0 + nrm(ks[20], (DEPTH, D_MODEL), 0.05),
        'ln2_b': nrm(ks[21], (DEPTH, D_MODEL), 0.02),
        'w_pg': nrm(ks[22], (DEPTH, D_MODEL, D_MODEL), D_MODEL ** -0.5),
        'w_pe': nrm(ks[23], (DEPTH, PLE_DIM, D_MODEL), PLE_DIM ** -0.5),
        'rpb_table': nrm(ks[24], (N_BUCKETS, 2 * H_MIX), 0.5),
    }


def reference(x_prompt, x_sample, cache_kv, cache_logf, state_win, page_table, p_prompt, p_sample,
              w_in, b_forget, w_cmp1, w_cmp2, cmp_pos, w_branch, w_out, ln1_g, ln1_b,
              w_gate, w_up, w_down, ln2_g, ln2_b, w_pg, w_pe, rpb_table):
    n_p, seq = x_prompt.shape[0], x_prompt.shape[1]
    n_s = x_sample.shape[0]
    win_p = min(WINDOW, seq)
    win_s = state_win.shape[2]
    yp, ys = x_prompt, x_sample
    kv_p, kv_s, lf_p, lf_s, wn_p, wn_s = [], [], [], [], [], []
    for i in range(DEPTH):
        lp = {'w_in': w_in[i], 'b_forget': b_forget[i], 'w_cmp1': w_cmp1[i], 'w_cmp2': w_cmp2[i],
              'cmp_pos': cmp_pos[i], 'w_branch': w_branch[i], 'w_out': w_out[i],
              'ln1_g': ln1_g[i], 'ln1_b': ln1_b[i], 'w_gate': w_gate[i], 'w_up': w_up[i],
              'w_down': w_down[i], 'ln2_g': ln2_g[i], 'ln2_b': ln2_b[i], 'w_pg': w_pg[i], 'w_pe': w_pe[i]}
        prefix_p = jnp.zeros((n_p, WINDOW, C_WIN), yp.dtype)
        yp, rows, lf, wrow = trunk_layer(yp, p_prompt[i], None, None, prefix_p, lp, rpb_table)
        kv_p.append(rows)
        lf_p.append(lf)
        wn_p.append(wrow[:, seq - win_p:])
        past = cache_kv[i, page_table].reshape(n_s, -1, C_KV)
        past_lf = cache_logf[i, page_table].reshape(n_s, -1, H_MIX)
        prefix_s = jnp.pad(state_win[i], ((0, 0), (WINDOW - win_s, 0), (0, 0)))
        ys, rows, lf, wrow = trunk_layer(ys, p_sample[i], past, past_lf, prefix_s, lp, rpb_table)
        kv_s.append(rows)
        lf_s.append(lf)
        wn_s.append(jnp.concatenate([state_win[i], wrow.astype(state_win.dtype)], axis=1)[:, -win_s:])
    return (yp, ys, jnp.stack(kv_p), jnp.stack(kv_s), jnp.stack(lf_p), jnp.stack(lf_s), jnp.stack(wn_p), jnp.stack(wn_s))
```

```python
import functools
import math

import jax
import jax.numpy as jnp
import numpy as np
from jax import lax
from jax.experimental import pallas as pl
from jax.experimental.pallas import tpu as pltpu

F32 = jnp.float32
BF16 = jnp.bfloat16
I32 = jnp.int32

D_MODEL = 2048
HD = 64
H_MIX = 8
BRANCH_W = H_MIX * HD
KV_G = 2
G_Q = H_MIX // KV_G
TOPK_DSA = 256
CMP_LEN = 32
CMP_STRIDE = 16
SEL_LEN = 64
SEL_TOP = 16
WINDOW = 512
N_BUCKETS = 32
C_KV = 2880
C_WIN = 256
ALPHA = (2 * 4) ** 0.25
LN_EPS = 1e-5
NEG = -1e30
BIG = 1e30
SCALE = HD ** -0.5
INT_MIN = -2 ** 31

LANES = 128
SUBLANES = 8
QBLK = 128
PAGE = 128
VMEM_LIMIT = 48 * 1024 * 1024


def _t5_bucket_starts():
    n = np.arange(0, 4096)
    nf = np.maximum(n, 1).astype(np.float64)
    large = 16 + (np.log(nf / 16.0) / math.log(128 / 16) * 16.0).astype(np.int64)
    bucket = np.where(n < 16, n, np.minimum(large, N_BUCKETS - 1))
    return [int(np.argmax(bucket == b)) for b in range(N_BUCKETS)]


BUCKET_START = _t5_bucket_starts()


def _dot(a, b):
    return jnp.dot(a, b, preferred_element_type=F32)


def _dot_nt(a, b):
    return lax.dot_general(a, b, (((1,), (1,)), ((), ())), preferred_element_type=F32)


def _split_dot(x, m, pieces):
    out = None
    r = x
    for _ in range(pieces):
        p = r.astype(BF16)
        d = _dot(p, m)
        out = d if out is None else out + d
        r = r - p.astype(F32)
    return out


def _iota(shape, dim):
    return lax.broadcasted_iota(I32, shape, dim)


def _softplus_neg_abs(z):
    return jnp.log1p(jnp.exp(-jnp.abs(z)))


def _sigmoid(x):
    return 1.0 / (1.0 + jnp.exp(-x))


def _sort_key(x):
    b = lax.bitcast_convert_type(x, I32)
    return jnp.where(b < 0, b ^ jnp.int32(0x7FFFFFFF), b)


def _bias_from_table(rel, col):
    out = jnp.where(rel >= BUCKET_START[1], col(1), col(0))
    for b in range(2, N_BUCKETS):
        out = jnp.where(rel >= BUCKET_START[b], col(b), out)
    return out


def _flash_update(s, maskb, v, m_ref, l_ref, acc_ref):
    s = jnp.where(maskb, s, NEG)
    m_old = m_ref[...]
    m_new = jnp.maximum(m_old, jnp.max(s, axis=1, keepdims=True))
    p = jnp.where(maskb, jnp.exp(s - m_new), 0.0)
    alpha = jnp.exp(m_old - m_new)
    l_ref[...] = alpha * l_ref[...] + jnp.sum(p, axis=1, keepdims=True)
    acc_ref[...] = alpha * acc_ref[...] + _dot(p.astype(BF16), v)
    m_ref[...] = m_new


def _rowwise_topk_mask(x, k):
    r, n = x.shape
    key = _sort_key(x)
    tu = jnp.zeros((r, 1), I32)
    for bit in range(31, -1, -1):
        cand = tu | jnp.int32(INT_MIN if bit == 31 else (1 << bit))
        cnt = jnp.sum(jnp.where(key >= (cand ^ jnp.int32(INT_MIN)), 1.0, 0.0), axis=1, keepdims=True)
        tu = jnp.where(cnt >= k, cand, tu)
    thr = tu ^ jnp.int32(INT_MIN)
    gt = key > thr
    eq = key == thr
    need = k - jnp.sum(jnp.where(gt, 1.0, 0.0), axis=1, keepdims=True)
    su = jnp.where(_iota((n, n), 0) < _iota((n, n), 1), 1.0, 0.0).astype(BF16)
    pre = _dot(jnp.where(eq, 1.0, 0.0).astype(BF16), su)
    return jnp.where(gt, 1.0, jnp.where(eq, jnp.where(pre < need, 1.0, 0.0), 0.0))


def _stack_group_heads(q, g):
    parts = []
    lane = _iota((q.shape[0], LANES), 1)
    for r in range(G_Q):
        h = G_Q * g + r
        blk = q[:, LANES * (h // 2):LANES * (h // 2 + 1)]
        if h % 2 != g:
            blk = pltpu.roll(blk, HD, 1)
        parts.append(jnp.where((lane // HD) == g, blk, 0.0))
    return jnp.concatenate(parts, axis=0)


def _unstack_group_heads(o_g0, o_g1, rows):
    lane = _iota((rows, LANES), 1)
    tiles = []
    for p in range(H_MIX // 2):
        halves = []
        for h in (2 * p, 2 * p + 1):
            g, r = h // G_Q, h % G_Q
            blk = (o_g0 if g == 0 else o_g1)[rows * r:rows * (r + 1), :]
            if h % 2 != g:
                blk = pltpu.roll(blk, HD, 1)
            halves.append(blk)
        tiles.append(jnp.where(lane < HD, halves[0], halves[1]))
    return tiles


def _dense(lhs, pairs, extras, epilogue, out_dtypes, *, tm, tn, tk=None, name):
    m = lhs[0].shape[0]
    n = pairs[0][1].shape[1]
    n_lhs, n_pairs, n_ex, n_out = len(lhs), len(pairs), len(extras), len(out_dtypes)
    if tk is None:
        nk = 1
    else:
        k_all = lhs[0].shape[1]
        assert all(x.shape[1] == k_all for x in lhs) and k_all % tk == 0
        nk = k_all // tk
    assert m % tm == 0
    grid = (m // tm, pl.cdiv(n, tn), nk)

    def body(*refs):
        lhs_refs = refs[:n_lhs]
        w_refs = refs[n_lhs:n_lhs + n_pairs]
        ex_refs = refs[n_lhs + n_pairs:n_lhs + n_pairs + n_ex]
        out_refs = refs[n_lhs + n_pairs + n_ex:n_lhs + n_pairs + n_ex + n_out]
        acc_refs = refs[n_lhs + n_pairs + n_ex + n_out:]
        xs = [r[...].astype(BF16) for r in lhs_refs]
        prods = [_dot(xs[a], w_refs[p][...]) for p, (a, _) in enumerate(pairs)]

        def finish(accs):
            res = epilogue(accs, [e[...] for e in ex_refs])
            for o, v in zip(out_refs, res):
                o[...] = v.astype(o.dtype)

        if nk == 1:
            finish(prods)
        else:
            kk = pl.program_id(2)

            @pl.when(kk == 0)
            def _():
                for a, p in zip(acc_refs, prods):
                    a[...] = p

            @pl.when(kk > 0)
            def _():
                for a, p in zip(acc_refs, prods):
                    a[...] += p

            @pl.when(kk == nk - 1)
            def _():
                finish([a[...] for a in acc_refs])

    in_specs = []
    for x in lhs:
        kx = x.shape[1] if tk is None else tk
        in_specs.append(pl.BlockSpec((tm, kx), lambda i, j, k: (i, k)))
    for a, w in pairs:
        kx = w.shape[0] if tk is None else tk
        in_specs.append(pl.BlockSpec((kx, tn), lambda i, j, k: (k, j)))
    for kind, arr in extras:
        if kind == 'tile':
            in_specs.append(pl.BlockSpec((tm, tn), lambda i, j, k: (i, j)))
        else:
            in_specs.append(pl.BlockSpec((1, tn), lambda i, j, k: (0, j)))
    out_specs = [pl.BlockSpec((tm, tn), lambda i, j, k: (i, j)) for _ in out_dtypes]
    out_shape = [jax.ShapeDtypeStruct((m, n), dt) for dt in out_dtypes]
    scratch = [] if nk == 1 else [pltpu.VMEM((tm, tn), F32) for _ in pairs]
    outs = pl.pallas_call(
        body, grid=grid, in_specs=in_specs, out_specs=out_specs, out_shape=out_shape, scratch_shapes=scratch,
        compiler_params=pltpu.CompilerParams(dimension_semantics=("parallel", "parallel", "arbitrary"),
                                             vmem_limit_bytes=VMEM_LIMIT),
        name=name,
    )(*lhs, *[w for _, w in pairs], *[arr for _, arr in extras])
    return outs


def _row_tile(m):
    return 512 if m % 512 == 0 else (256 if m % 256 == 0 else m)


def _k_tile(k):
    return 512 if k % 512 == 0 and k > 512 else None


def _ep_identity(accs, ex):
    return [accs[0]]


def _ep_small(accs, ex):
    z = accs[0] + ex[0]
    return [accs[0], jnp.minimum(z, 0.0) - _softplus_neg_abs(z)]


def _ep_merge(accs, ex):
    out = None
    for n in range(4):
        t = _sigmoid(accs[n]) * accs[4 + n]
        out = t if out is None else out + t
    return [out]


def _ep_ln_residual(accs, ex):
    y = ALPHA * ex[0] + accs[0]
    mu = jnp.mean(y, axis=-1, keepdims=True)
    var = jnp.mean(jnp.square(y - mu), axis=-1, keepdims=True)
    return [(y - mu) * lax.rsqrt(var + LN_EPS) * ex[1] + ex[2]]


def _ep_swiglu(accs, ex):
    return [accs[0] * _sigmoid(accs[0]) * accs[1]]


def _ep_ple_gate(accs, ex):
    return [ex[0] + _sigmoid(accs[0]) * accs[1]]


def _cumsum_kernel(x_ref, cum_ref, cumt_ref, *, t, tc):
    tri = jnp.where(_iota((tc, tc), 1) <= _iota((tc, tc), 0), 1.0, 0.0).astype(BF16)
    carry = jnp.zeros((1, LANES), F32)
    for c in range(t // tc):
        x = x_ref[c * tc:(c + 1) * tc, :]
        cum = _split_dot_left(tri, x) + carry
        cum_ref[c * tc:(c + 1) * tc, :] = cum
        cumt_ref[:, c * tc:(c + 1) * tc] = jnp.transpose(cum)[:SUBLANES, :]
        carry = cum[tc - 1:tc, :]


def _split_dot_left(m, x):
    out = None
    r = x
    for _ in range(3):
        p = r.astype(BF16)
        d = _dot(m, p)
        out = d if out is None else out + d
        r = r - p.astype(F32)
    return out


def _cumsum_call(lsig, b, t):
    tc = 256 if t % 256 == 0 else t
    return pl.pallas_call(
        functools.partial(_cumsum_kernel, t=t, tc=tc),
        grid=(b,),
        in_specs=[pl.BlockSpec((t, LANES), lambda i: (i, 0))],
        out_specs=[pl.BlockSpec((t, LANES), lambda i: (i, 0)), pl.BlockSpec((None, SUBLANES, t), lambda i: (i, 0, 0))],
        out_shape=[jax.ShapeDtypeStruct((b * t, LANES), F32), jax.ShapeDtypeStruct((b, SUBLANES, t), F32)],
        compiler_params=pltpu.CompilerParams(dimension_semantics=("arbitrary",), vmem_limit_bytes=VMEM_LIMIT),
        name="forget_cumsum",
    )(lsig)


def _sb_kernel(q_ref, k_ref, v_ref, o_ref):
    i = pl.program_id(2)
    lane = _iota((QBLK, LANES), 1)
    tri = jnp.where(_iota((QBLK, QBLK), 0) >= _iota((QBLK, QBLK), 1), 1.0, 0.0).astype(BF16)
    tpos = i * QBLK + _iota((QBLK, QBLK), 0)
    kin = _iota((QBLK, QBLK), 1)
    q = q_ref[...] * SCALE
    accs = []
    for hh in range(2):
        qh = jnp.where((lane // HD) == hh, q, 0.0).astype(BF16)

        def step(jj, carry, qh=qh):
            acc, cs = carry
            j = i - jj
            off = pl.multiple_of(j * QBLK, QBLK)
            kc = k_ref[pl.ds(off, QBLK), :].astype(BF16)
            vc = v_ref[pl.ds(off, QBLK), :].astype(BF16)
            z = _dot_nt(qh, kc)
            sp = _softplus_neg_abs(z)
            valid = (off + kin) < tpos
            u = jnp.where(valid, jnp.minimum(-z, 0.0) - sp, 0.0)
            rc = _split_dot(u, tri, 2) + cs
            logw = (jnp.minimum(z, 0.0) - sp) + rc - u
            a = jnp.where(valid, jnp.exp(logw), 0.0)
            return acc + _dot(a.astype(BF16), vc), rc[:, 0:1]

        acc, _ = lax.fori_loop(0, i + 1, step, (jnp.zeros((QBLK, LANES), F32), jnp.zeros((QBLK, 1), F32)))
        accs.append(acc)
    o_ref[...] = jnp.where(lane < HD, accs[0], accs[1])


def _sb_call(qall, rows, b, t):
    nq = t // QBLK
    return pl.pallas_call(
        _sb_kernel,
        grid=(b, H_MIX // 2, nq),
        in_specs=[pl.BlockSpec((QBLK, LANES), lambda bb, p, i: (bb * (t // QBLK) + i, p)),
                  pl.BlockSpec((t, LANES), lambda bb, p, i: (bb, p)),
                  pl.BlockSpec((t, LANES), lambda bb, p, i: (bb, 4 + p))],
        out_specs=pl.BlockSpec((QBLK, LANES), lambda bb, p, i: (bb * (t // QBLK) + i, p)),
        out_shape=jax.ShapeDtypeStruct((b * t, BRANCH_W), F32),
        compiler_params=pltpu.CompilerParams(dimension_semantics=("arbitrary",) * 3, vmem_limit_bytes=VMEM_LIMIT),
        name="prompt_stickbreaking",
    )(qall, rows, rows)


def _fox_kernel(q_ref, k_ref, v_ref, cum_ref, cumt_ref, o_ref, m_ref, l_ref, acc_ref):
    p = pl.program_id(1)
    i = pl.program_id(2)
    lane = _iota((QBLK, LANES), 1)
    tpos = i * QBLK + _iota((QBLK, QBLK), 0)
    kin = _iota((QBLK, QBLK), 1)
    q = q_ref[...] * SCALE
    cum = cum_ref[...]
    outs = []
    for hh in range(2):
        qh = jnp.where((lane // HD) == hh, q, 0.0).astype(BF16)
        h = 2 * p + hh
        c_t = jnp.sum(jnp.where(lane == h, cum, 0.0), axis=1, keepdims=True)
        m_ref[...] = jnp.full((QBLK, 1), NEG, F32)
        l_ref[...] = jnp.zeros((QBLK, 1), F32)
        acc_ref[...] = jnp.zeros((QBLK, LANES), F32)

        def step(j, _, qh=qh, c_t=c_t, h=h):
            off = pl.multiple_of(j * QBLK, QBLK)
            kc = k_ref[pl.ds(off, QBLK), :].astype(BF16)
            vc = v_ref[pl.ds(off, QBLK), :].astype(BF16)
            c_all = cumt_ref[:, pl.ds(off, QBLK)]
            c_s = jnp.sum(jnp.where(_iota((SUBLANES, QBLK), 0) == h, c_all, 0.0), axis=0, keepdims=True)
            s = _dot_nt(qh, kc) + c_t - c_s
            _flash_update(s, (off + kin) <= tpos, vc, m_ref, l_ref, acc_ref)
            return 0

        lax.fori_loop(0, i + 1, step, 0)
        outs.append(acc_ref[...] / l_ref[...])
    o_ref[...] = jnp.where(lane < HD, outs[0], outs[1])


def _fox_call(qall, rows, cum, cumt, b, t):
    nq = t // QBLK
    return pl.pallas_call(
        _fox_kernel,
        grid=(b, H_MIX // 2, nq),
        in_specs=[pl.BlockSpec((QBLK, LANES), lambda bb, p, i: (bb * (t // QBLK) + i, 4 + p)),
                  pl.BlockSpec((t, LANES), lambda bb, p, i: (bb, 8 + p)),
                  pl.BlockSpec((t, LANES), lambda bb, p, i: (bb, 12 + p)),
                  pl.BlockSpec((QBLK, LANES), lambda bb, p, i: (bb * (t // QBLK) + i, 0)),
                  pl.BlockSpec((None, SUBLANES, t), lambda bb, p, i: (bb, 0, 0))],
        out_specs=pl.BlockSpec((QBLK, LANES), lambda bb, p, i: (bb * (t // QBLK) + i, p)),
        out_shape=jax.ShapeDtypeStruct((b * t, BRANCH_W), F32),
        scratch_shapes=[pltpu.VMEM((QBLK, 1), F32), pltpu.VMEM((QBLK, 1), F32), pltpu.VMEM((QBLK, LANES), F32)],
        compiler_params=pltpu.CompilerParams(dimension_semantics=("arbitrary",) * 3, vmem_limit_bytes=VMEM_LIMIT),
        name="prompt_fox",
    )(qall, rows, rows, cum, cumt)


def _build_bias_tiles(tab_ref, tile_ref, head0):
    rel0 = _iota((QBLK, QBLK), 0) - _iota((QBLK, QBLK), 1)
    for d in range(2):
        rel = rel0 + QBLK * d
        for g in range(KV_G):
            for r in range(G_Q):
                h = head0 + G_Q * g + r
                tile_ref[g, d, QBLK * r:QBLK * (r + 1), :] = _bias_from_table(rel, lambda b, h=h: tab_ref[b, h])


def _far_bias_col(tab_ref, head0, g):
    row = _iota((G_Q * QBLK, 1), 0) // QBLK
    col = jnp.zeros((G_Q * QBLK, 1), F32)
    for r in range(G_Q):
        col = jnp.where(row == r, tab_ref[N_BUCKETS - 1, head0 + G_Q * g + r], col)
    return col


def _tile4(x):
    return jnp.concatenate([x] * G_Q, axis=0)


def _dsa_kernel(tab_ref, iq_ref, sm_ref, q_ref, kix_ref, k_ref, v_ref, o_ref,
                keys_ref, tile_ref, m_ref, l_ref, acc_ref, ceq_ref, *, k_top):
    b = pl.program_id(0)
    i = pl.program_id(1)

    @pl.when((b == 0) & (i == 0))
    def _():
        _build_bias_tiles(tab_ref, tile_ref, 0)

    tpos = i * QBLK + _iota((QBLK, QBLK), 0)
    kin = _iota((QBLK, QBLK), 1)

    iq = iq_ref[...]
    parts = []
    for h in range(H_MIX):
        blk = iq[:, LANES * (h // 2):LANES * (h // 2 + 1)]
        if h % 2 == 1:
            blk = pltpu.roll(blk, HD, 1)
        parts.append(blk)
    iq_all = jnp.concatenate(parts, axis=0).astype(BF16)
    sm = sm_ref[...]
    lane = _iota((QBLK, LANES), 1)
    iw = [jnp.sum(jnp.where(lane == SUBLANES + h, sm, 0.0), axis=1, keepdims=True) for h in range(H_MIX)]

    def score_step(j, _):
        off = pl.multiple_of(j * QBLK, QBLK)
        kc = kix_ref[pl.ds(off, QBLK), :].astype(BF16)
        zz = _dot_nt(iq_all, kc)
        isc = jnp.maximum(zz[0:QBLK], 0.0) * iw[0]
        for h in range(1, H_MIX):
            isc = isc + jnp.maximum(zz[QBLK * h:QBLK * (h + 1)], 0.0) * iw[h]
        keys_ref[:, pl.ds(off, QBLK)] = jnp.where((off + kin) <= tpos, _sort_key(isc), jnp.int32(INT_MIN))
        return 0

    lax.fori_loop(0, i + 1, score_step, 0)

    def count(pred):
        def cstep(j, c):
            off = pl.multiple_of(j * QBLK, QBLK)
            return c + jnp.where(pred(keys_ref[:, pl.ds(off, QBLK)]), 1.0, 0.0)
        c = lax.fori_loop(0, i + 1, cstep, jnp.zeros((QBLK, QBLK), F32))
        return jnp.sum(c, axis=1, keepdims=True)

    def bit_step(it, tu):
        cand = tu | lax.shift_left(jnp.int32(1), 31 - it)
        cs = cand ^ jnp.int32(INT_MIN)
        return jnp.where(count(lambda kk: kk >= cs) >= k_top, cand, tu)

    tu = lax.fori_loop(0, 32, bit_step, jnp.zeros((QBLK, 1), I32))
    thr = tu ^ jnp.int32(INT_MIN)
    need = k_top - count(lambda kk: kk > thr)

    su = jnp.where(_iota((QBLK, QBLK), 0) < _iota((QBLK, QBLK), 1), 1.0, 0.0).astype(BF16)
    q = q_ref[...] * SCALE
    qg = [_stack_group_heads(q, g).astype(BF16) for g in range(KV_G)]
    far = [_far_bias_col(tab_ref, 0, g) for g in range(KV_G)]
    ceq_ref[...] = jnp.zeros((QBLK, 1), F32)
    for g in range(KV_G):
        m_ref[g] = jnp.full((G_Q * QBLK, 1), NEG, F32)
        l_ref[g] = jnp.zeros((G_Q * QBLK, 1), F32)
        acc_ref[g] = jnp.zeros((G_Q * QBLK, LANES), F32)

    def attend(j, bias_of):
        off = pl.multiple_of(j * QBLK, QBLK)
        kk = keys_ref[:, pl.ds(off, QBLK)]
        eq = kk == thr
        eqf = jnp.where(eq, 1.0, 0.0)
        pre = _dot(eqf.astype(BF16), su) + ceq_ref[...]
        ceq_ref[...] = ceq_ref[...] + jnp.sum(eqf, axis=1, keepdims=True)
        sel = jnp.where(kk > thr, 1.0, jnp.where(eq, jnp.where(pre < need, 1.0, 0.0), 0.0))
        sel = jnp.where((off + kin) <= tpos, sel, 0.0)
        maskb = _tile4(sel) > 0.5
        kc = k_ref[pl.ds(off, QBLK), :].astype(BF16)
        vc = v_ref[pl.ds(off, QBLK), :].astype(BF16)
        for g in range(KV_G):
            s = _dot_nt(qg[g], kc) + bias_of(g)
            _flash_update(s, maskb, vc, m_ref.at[g], l_ref.at[g], acc_ref.at[g])

    def far_step(j, _):
        attend(j, lambda g: jnp.where(j == i - 1, tile_ref[g, 1], far[g]))
        return 0

    lax.fori_loop(0, i, far_step, 0)
    attend(i, lambda g: tile_ref[g, 0])
    outs = []
    for g in range(KV_G):
        outs.append(acc_ref[g] / l_ref[g])
    for p, tile in enumerate(_unstack_group_heads(outs[0], outs[1], QBLK)):
        o_ref[:, LANES * p:LANES * (p + 1)] = tile


def _dsa_call(tab, qall, small, aux, rows, b, t):
    nq = t // QBLK
    k_top = min(TOPK_DSA, t // 4)
    qmap = lambda col: (lambda bb, i: (bb * (t // QBLK) + i, col))
    return pl.pallas_call(
        functools.partial(_dsa_kernel, k_top=k_top),
        grid=(b, nq),
        in_specs=[pl.BlockSpec(memory_space=pltpu.SMEM),
                  pl.BlockSpec((QBLK, BRANCH_W), qmap(3)),
                  pl.BlockSpec((QBLK, LANES), qmap(0)),
                  pl.BlockSpec((QBLK, BRANCH_W), qmap(2)),
                  pl.BlockSpec((t, LANES), lambda bb, i: (bb, 4)),
                  pl.BlockSpec((t, LANES), lambda bb, i: (bb, 16)),
                  pl.BlockSpec((t, LANES), lambda bb, i: (bb, 17))],
        out_specs=pl.BlockSpec((QBLK, BRANCH_W), qmap(0)),
        out_shape=jax.ShapeDtypeStruct((b * t, BRANCH_W), F32),
        scratch_shapes=[pltpu.VMEM((QBLK, t), I32),
                        pltpu.VMEM((KV_G, 2, G_Q * QBLK, QBLK), F32),
                        pltpu.VMEM((KV_G, G_Q * QBLK, 1), F32),
                        pltpu.VMEM((KV_G, G_Q * QBLK, 1), F32),
                        pltpu.VMEM((KV_G, G_Q * QBLK, LANES), F32),
                        pltpu.VMEM((QBLK, 1), F32)],
        compiler_params=pltpu.CompilerParams(dimension_semantics=("arbitrary",) * 2, vmem_limit_bytes=VMEM_LIMIT),
        name="prompt_dsa",
    )(tab, qall, small, qall, aux, rows, rows)


def _compress_rows(src_refs, n, w1_ref, pos_ref, w2_ref):
    outs = []
    for s, src_ref in enumerate(src_refs):
        u = []
        for a in range(CMP_LEN // CMP_STRIDE):
            acc = None
            for r in range(CMP_STRIDE):
                l = CMP_STRIDE * a + r
                x = src_ref[pl.ds(r, n, stride=CMP_STRIDE), :] + pos_ref[l:l + 1, :]
                d = _dot(x.astype(BF16), w1_ref[s, l])
                acc = d if acc is None else acc + d
            u.append(acc)
        hid_pre = u[0] + pltpu.roll(u[1], n - 1, 0)
        hid = jax.nn.gelu(hid_pre)
        outs.append(_dot(hid.astype(BF16), w2_ref[s]))
    return jnp.concatenate(outs, axis=1)


def _compress_kernel(k_ref, v_ref, w1_ref, pos_ref, w2_ref, o_ref, *, n):
    o_ref[...] = _compress_rows((k_ref, v_ref), n, w1_ref, pos_ref, w2_ref)


def _compress_call(aux, w1, pos, w2, b, t):
    n = t // CMP_STRIDE
    return pl.pallas_call(
        functools.partial(_compress_kernel, n=n),
        grid=(b,),
        in_specs=[pl.BlockSpec((t, LANES), lambda i: (i, 0)),
                  pl.BlockSpec((t, LANES), lambda i: (i, 1)),
                  pl.BlockSpec((2, CMP_LEN, LANES, LANES), lambda i: (0, 0, 0, 0)),
                  pl.BlockSpec((CMP_LEN, LANES), lambda i: (0, 0)),
                  pl.BlockSpec((2, LANES, LANES), lambda i: (0, 0, 0))],
        out_specs=pl.BlockSpec((None, n, 2 * LANES), lambda i: (i, 0, 0)),
        out_shape=jax.ShapeDtypeStruct((b, n, 2 * LANES), F32),
        compiler_params=pltpu.CompilerParams(dimension_semantics=("arbitrary",), vmem_limit_bytes=VMEM_LIMIT),
        name="prompt_nsa_compress",
    )(aux, aux, w1, pos, w2)


def _nsa_kernel(tab_ref, q_ref, sm_ref, cmp_ref, ks_ref, vs_ref, kw_ref, vw_ref, o_ref,
                tile_ref, m_ref, l_ref, acc_ref, *, n_cmp, n_sel, n_top):
    b = pl.program_id(0)
    i = pl.program_id(1)

    @pl.when((b == 0) & (i == 0))
    def _():
        _build_bias_tiles(tab_ref, tile_ref, H_MIX)

    ncp = cmp_ref.shape[0]
    tpos = i * QBLK + _iota((QBLK, QBLK), 0)
    kin = _iota((QBLK, QBLK), 1)
    q = q_ref[...] * SCALE
    qg = [_stack_group_heads(q, g).astype(BF16) for g in range(KV_G)]
    far = [_far_bias_col(tab_ref, H_MIX, g) for g in range(KV_G)]
    cmp = cmp_ref[...]
    kcmp = cmp[:, 0:LANES].astype(BF16)
    vcmp = cmp[:, LANES:2 * LANES].astype(BF16)

    t_c = i * QBLK + _iota((QBLK, ncp), 0)
    n_idx = _iota((QBLK, ncp), 1)
    rel_c = t_c - (n_idx * CMP_STRIDE + (CMP_LEN - 1))
    ok_c = (rel_c >= 0) & (n_idx < n_cmp)
    ok_c4 = _tile4(jnp.where(ok_c, 1.0, 0.0)) > 0.5
    spread = jnp.where((_iota((ncp, LANES), 0) >= 4 * _iota((ncp, LANES), 1) - 1)
                       & (_iota((ncp, LANES), 0) <= 4 * _iota((ncp, LANES), 1) + 3), 1.0, 0.0).astype(BF16)
    blk = _iota((QBLK, LANES), 1)
    tb = (i * QBLK + _iota((QBLK, LANES), 0)) // SEL_LEN
    forced = (blk == 0) | (blk == tb) | (blk == tb - 1)
    o_c = []
    sel_blk = []
    for g in range(KV_G):
        bias = jnp.concatenate(
            [_bias_from_table(rel_c, lambda bb, h=H_MIX + G_Q * g + r: tab_ref[bb, h]) for r in range(G_Q)], axis=0)
        s = jnp.where(ok_c4, _dot_nt(qg[g], kcmp) + bias, NEG)
        mx = jnp.max(s, axis=1, keepdims=True)
        p = jnp.where(ok_c4, jnp.exp(s - mx), 0.0)
        den = jnp.sum(p, axis=1, keepdims=True)
        pc = p / jnp.where(den == 0.0, 1.0, den)
        o_c.append(_dot(pc.astype(BF16), vcmp))
        pcs = pc[0:QBLK] + pc[QBLK:2 * QBLK] + pc[2 * QBLK:3 * QBLK] + pc[3 * QBLK:4 * QBLK]
        imp = _split_dot(pcs, spread, 3)
        score = jnp.where(forced, BIG, jnp.where(blk <= tb, imp, NEG))
        score = jnp.where(blk < n_sel, score, -3e38)
        sel_blk.append(_rowwise_topk_mask(score, n_top).astype(BF16))

    def init_state():
        for g in range(KV_G):
            m_ref[g] = jnp.full((G_Q * QBLK, 1), NEG, F32)
            l_ref[g] = jnp.zeros((G_Q * QBLK, 1), F32)
            acc_ref[g] = jnp.zeros((G_Q * QBLK, LANES), F32)

    def finish_state():
        return [acc_ref[g] / l_ref[g] for g in range(KV_G)]

    init_state()

    def sel_attend(j, bias_of):
        off = pl.multiple_of(j * QBLK, QBLK)
        expand = jnp.where(_iota((LANES, QBLK), 0) == (off + _iota((LANES, QBLK), 1)) // SEL_LEN, 1.0, 0.0).astype(BF16)
        causal = (off + kin) <= tpos
        kc = ks_ref[pl.ds(off, QBLK), :].astype(BF16)
        vc = vs_ref[pl.ds(off, QBLK), :].astype(BF16)
        for g in range(KV_G):
            selk = jnp.where(causal, _dot(sel_blk[g], expand), 0.0)
            s = _dot_nt(qg[g], kc) + bias_of(g)
            _flash_update(s, _tile4(selk) > 0.5, vc, m_ref.at[g], l_ref.at[g], acc_ref.at[g])

    def sel_far(j, _):
        sel_attend(j, lambda g: jnp.where(j == i - 1, tile_ref[g, 1], far[g]))
        return 0

    lax.fori_loop(0, i, sel_far, 0)
    sel_attend(i, lambda g: tile_ref[g, 0])
    o_s = finish_state()

    init_state()

    def win_attend(j, bias_of):
        off = pl.multiple_of(j * QBLK, QBLK)
        rel = tpos - (off + kin)
        maskb = _tile4(jnp.where((rel >= 0) & (rel <= WINDOW), 1.0, 0.0)) > 0.5
        kc = kw_ref[pl.ds(off, QBLK), :].astype(BF16)
        vc = vw_ref[pl.ds(off, QBLK), :].astype(BF16)
        for g in range(KV_G):
            s = _dot_nt(qg[g], kc) + bias_of(g)
            _flash_update(s, maskb, vc, m_ref.at[g], l_ref.at[g], acc_ref.at[g])

    def win_far(j, _):
        win_attend(j, lambda g: jnp.where(j == i - 1, tile_ref[g, 1], far[g]))
        return 0

    lax.fori_loop(jnp.maximum(i - WINDOW // QBLK, 0), i, win_far, 0)
    win_attend(i, lambda g: tile_ref[g, 0])
    o_w = finish_state()

    sm = sm_ref[...]
    lane = _iota((QBLK, LANES), 1)
    outs = []
    for g in range(KV_G):
        cols = []
        for c in range(3):
            parts = []
            for r in range(G_Q):
                idx = 2 * SUBLANES + c * H_MIX + G_Q * g + r
                parts.append(_sigmoid(jnp.sum(jnp.where(lane == idx, sm, 0.0), axis=1, keepdims=True)))
            cols.append(jnp.concatenate(parts, axis=0))
        outs.append(cols[0] * o_c[g] + cols[1] * o_s[g] + cols[2] * o_w[g])
    for p, tile in enumerate(_unstack_group_heads(outs[0], outs[1], QBLK)):
        o_ref[:, LANES * p:LANES * (p + 1)] = tile


def _nsa_call(tab, qall, small, comp, aux, win, b, t):
    nq = t // QBLK
    n_cmp = (t - CMP_LEN) // CMP_STRIDE + 1
    n_sel = -(-t // SEL_LEN)
    n_top = min(SEL_TOP, n_sel)
    assert n_sel <= LANES
    qmap = lambda col: (lambda bb, i: (bb * (t // QBLK) + i, col))
    return pl.pallas_call(
        functools.partial(_nsa_kernel, n_cmp=n_cmp, n_sel=n_sel, n_top=n_top),
        grid=(b, nq),
        in_specs=[pl.BlockSpec(memory_space=pltpu.SMEM),
                  pl.BlockSpec((QBLK, BRANCH_W), qmap(4)),
                  pl.BlockSpec((QBLK, LANES), qmap(0)),
                  pl.BlockSpec((None, t // CMP_STRIDE, 2 * LANES), lambda bb, i: (bb, 0, 0)),
                  pl.BlockSpec((t, LANES), lambda bb, i: (bb, 2)),
                  pl.BlockSpec((t, LANES), lambda bb, i: (bb, 3)),
                  pl.BlockSpec((t, LANES), lambda bb, i: (bb, 0)),
                  pl.BlockSpec((t, LANES), lambda bb, i: (bb, 1))],
        out_specs=pl.BlockSpec((QBLK, BRANCH_W), qmap(0)),
        out_shape=jax.ShapeDtypeStruct((b * t, BRANCH_W), F32),
        scratch_shapes=[pltpu.VMEM((KV_G, 2, G_Q * QBLK, QBLK), F32),
                        pltpu.VMEM((KV_G, G_Q * QBLK, 1), F32),
                        pltpu.VMEM((KV_G, G_Q * QBLK, 1), F32),
                        pltpu.VMEM((KV_G, G_Q * QBLK, LANES), F32)],
        compiler_params=pltpu.CompilerParams(dimension_semantics=("arbitrary",) * 2, vmem_limit_bytes=VMEM_LIMIT),
        name="prompt_nsa",
    )(tab, qall, small, comp, aux, aux, win, win)


def _s_sbfox_kernel(pt_ref, qsb_ref, qfx_ref, kn_ref, vn_ref, lfn_ref, kv_ref, lft_ref, osb_ref, ofx_ref,
                    accsb_ref, cs_ref, m_ref, l_ref, accfx_ref, cd_ref, *, n_pages):
    j = pl.program_id(1)
    lane = _iota((H_MIX, BRANCH_W), 1)
    row = _iota((H_MIX, BRANCH_W), 0)
    bd = (lane // HD) == row
    qsb = jnp.where(bd, qsb_ref[...], 0.0) * SCALE
    qfx = jnp.where(bd, qfx_ref[...], 0.0) * SCALE

    @pl.when(j == 0)
    def _():
        accsb_ref[...] = jnp.zeros((H_MIX, BRANCH_W), F32)
        cs_ref[...] = jnp.zeros((H_MIX, 1), F32)
        m_ref[...] = jnp.sum(qfx * kn_ref[...], axis=1, keepdims=True)
        l_ref[...] = jnp.ones((H_MIX, 1), F32)
        accfx_ref[...] = jnp.broadcast_to(vn_ref[...], (H_MIX, BRANCH_W))
        cd_ref[...] = lfn_ref[...]

    kv = kv_ref[...]
    kr = _iota((PAGE, PAGE), 0)
    kc = _iota((PAGE, PAGE), 1)
    tri_incl = jnp.where(kr >= kc, 1.0, 0.0).astype(BF16)
    tri_excl = jnp.where(kr > kc, 1.0, 0.0).astype(BF16)

    z = _dot_nt(qsb.astype(BF16), kv[:, 0:BRANCH_W].astype(BF16))
    sp = _softplus_neg_abs(z)
    u = jnp.minimum(-z, 0.0) - sp
    rc = _split_dot(u, tri_incl, 2) + cs_ref[...]
    a = jnp.exp((jnp.minimum(z, 0.0) - sp) + rc - u)
    accsb_ref[...] += _dot(a.astype(BF16), kv[:, BRANCH_W:2 * BRANCH_W].astype(BF16))
    cs_ref[...] = rc[:, 0:1]

    lf = lft_ref[...]
    dpage = _split_dot(lf, tri_excl, 3) + cd_ref[...]
    s = _dot_nt(qfx.astype(BF16), kv[:, 2 * BRANCH_W:3 * BRANCH_W].astype(BF16)) + dpage
    _flash_update(s, kc[0:H_MIX, :] >= 0, kv[:, 3 * BRANCH_W:4 * BRANCH_W].astype(BF16), m_ref, l_ref, accfx_ref)
    cd_ref[...] = cd_ref[...] + jnp.sum(lf, axis=1, keepdims=True)

    @pl.when(j == n_pages - 1)
    def _():
        osb_ref[...] = jnp.sum(jnp.where(bd, accsb_ref[...], 0.0), axis=0, keepdims=True)
        ofx_ref[...] = jnp.sum(jnp.where(bd, accfx_ref[...] / l_ref[...], 0.0), axis=0, keepdims=True)


def _s_sbfox_call(layer, page_table, qsb, qfx, kn, vn, lfn, cache_kv, cache_lft):
    bs, n_pages = page_table.shape
    row_spec = pl.BlockSpec((None, 1, BRANCH_W), lambda b, j, pt: (b, 0, 0))
    grid_spec = pltpu.PrefetchScalarGridSpec(
        num_scalar_prefetch=1, grid=(bs, n_pages),
        in_specs=[row_spec, row_spec, row_spec, row_spec,
                  pl.BlockSpec((None, H_MIX, 1), lambda b, j, pt: (b, 0, 0)),
                  pl.BlockSpec((None, None, PAGE, 4 * BRANCH_W), lambda b, j, pt: (layer, pt[b, n_pages - 1 - j], 0, 0)),
                  pl.BlockSpec((None, None, H_MIX, PAGE), lambda b, j, pt: (layer, pt[b, n_pages - 1 - j], 0, 0))],
        out_specs=[row_spec, row_spec],
        scratch_shapes=[pltpu.VMEM((H_MIX, BRANCH_W), F32), pltpu.VMEM((H_MIX, 1), F32), pltpu.VMEM((H_MIX, 1), F32),
                        pltpu.VMEM((H_MIX, 1), F32), pltpu.VMEM((H_MIX, BRANCH_W), F32), pltpu.VMEM((H_MIX, 1), F32)])
    return pl.pallas_call(
        functools.partial(_s_sbfox_kernel, n_pages=n_pages),
        grid_spec=grid_spec,
        out_shape=[jax.ShapeDtypeStruct((bs, 1, BRANCH_W), F32)] * 2,
        compiler_params=pltpu.CompilerParams(dimension_semantics=("arbitrary",) * 2, vmem_limit_bytes=VMEM_LIMIT),
        name="sample_sb_fox",
    )(page_table, qsb, qfx, kn, vn, lfn, cache_kv, cache_lft)


def _s_index_kernel(pt_ref, iq_ref, iw_ref, blk_ref, w1_ref, pos_ref, w2_ref, isc_ref, cmp_ref,
                    kstash_ref, vstash_ref, *, n_pages):
    j = pl.program_id(1)
    x = blk_ref[...]
    rows = pl.ds(pl.multiple_of(j * PAGE, PAGE), PAGE)
    kstash_ref[rows, :] = x[:, HD:HD + LANES]
    vstash_ref[rows, :] = x[:, HD + LANES:HD + 2 * LANES]
    zz = _dot_nt(iq_ref[...].astype(BF16), x.astype(BF16))
    isc_ref[...] = jnp.sum(jnp.maximum(zz, 0.0) * iw_ref[...], axis=0, keepdims=True)

    @pl.when(j == n_pages - 1)
    def _():
        cmp_ref[...] = _compress_rows((kstash_ref, vstash_ref), n_pages * PAGE // CMP_STRIDE, w1_ref, pos_ref, w2_ref)


def _s_index_call(layer, page_table, iq384, iw, cache_kv, w1, pos, w2):
    bs, n_pages = page_table.shape
    n = n_pages * PAGE // CMP_STRIDE
    grid_spec = pltpu.PrefetchScalarGridSpec(
        num_scalar_prefetch=1, grid=(bs, n_pages),
        in_specs=[pl.BlockSpec((None, H_MIX, 3 * LANES), lambda b, j, pt: (b, 0, 0)),
                  pl.BlockSpec((None, H_MIX, 1), lambda b, j, pt: (b, 0, 0)),
                  pl.BlockSpec((None, None, PAGE, 3 * LANES), lambda b, j, pt: (layer, pt[b, j], 0, 6)),
                  pl.BlockSpec((2, CMP_LEN, LANES, LANES), lambda b, j, pt: (0, 0, 0, 0)),
                  pl.BlockSpec((CMP_LEN, LANES), lambda b, j, pt: (0, 0)),
                  pl.BlockSpec((2, LANES, LANES), lambda b, j, pt: (0, 0, 0))],
        out_specs=[pl.BlockSpec((None, None, 1, PAGE), lambda b, j, pt: (b, j, 0, 0)),
                   pl.BlockSpec((None, n, 2 * LANES), lambda b, j, pt: (b, 0, 0))],
        scratch_shapes=[pltpu.VMEM((n_pages * PAGE, LANES), F32), pltpu.VMEM((n_pages * PAGE, LANES), F32)])
    return pl.pallas_call(
        functools.partial(_s_index_kernel, n_pages=n_pages),
        grid_spec=grid_spec,
        out_shape=[jax.ShapeDtypeStruct((bs, n_pages, 1, PAGE), F32), jax.ShapeDtypeStruct((bs, n, 2 * LANES), F32)],
        compiler_params=pltpu.CompilerParams(dimension_semantics=("arbitrary",) * 2, vmem_limit_bytes=VMEM_LIMIT),
        name="sample_index_compress",
    )(page_table, iq384, iw, cache_kv, w1, pos, w2)


def _total(x):
    return jnp.sum(jnp.sum(x, axis=1, keepdims=True), axis=0, keepdims=True)


def _s_select_kernel(tabt_ref, isc_ref, iq_ref, iw_ref, kixn_ref, qn_ref, cmp_ref, win_ref, winn_ref,
                     dsel_ref, nsel_ref, misc_ref, oc_ref, ow_ref, *, past, k_top, n_cmp, n_sel, n_top):
    n_pages = isc_ref.shape[0]
    t = past

    zz = jnp.sum(iq_ref[...] * kixn_ref[...], axis=1, keepdims=True)
    isc_self = jnp.sum(jnp.maximum(zz, 0.0) * iw_ref[...], axis=0, keepdims=True)
    keys = _sort_key(isc_ref[...])
    kself = _sort_key(isc_self)
    tu = jnp.zeros((1, 1), I32)
    for bit in range(31, -1, -1):
        cand = tu | jnp.int32(INT_MIN if bit == 31 else (1 << bit))
        cs = cand ^ jnp.int32(INT_MIN)
        cnt = _total(jnp.where(keys >= cs, 1.0, 0.0)) + jnp.where(kself >= cs, 1.0, 0.0)
        tu = jnp.where(cnt >= k_top, cand, tu)
    thr = tu ^ jnp.int32(INT_MIN)
    gt = keys > thr
    eqf = jnp.where(keys == thr, 1.0, 0.0)
    need = k_top - (_total(jnp.where(gt, 1.0, 0.0)) + jnp.where(kself > thr, 1.0, 0.0))
    su = jnp.where(_iota((PAGE, PAGE), 0) < _iota((PAGE, PAGE), 1), 1.0, 0.0).astype(BF16)
    tl = jnp.where(_iota((n_pages, n_pages), 1) < _iota((n_pages, n_pages), 0), 1.0, 0.0).astype(BF16)
    rowtot = jnp.broadcast_to(jnp.sum(eqf, axis=1, keepdims=True), (n_pages, PAGE))
    pre = _dot(eqf.astype(BF16), su) + _dot(tl, rowtot.astype(BF16))
    dsel_ref[...] = jnp.where(gt, 1.0, jnp.where(eqf > 0.5, jnp.where(pre < need, 1.0, 0.0), 0.0))
    self_sel = jnp.where(kself > thr, 1.0, jnp.where(kself == thr, jnp.where(_total(eqf) < need, 1.0, 0.0), 0.0))
    misc_ref[...] = jnp.broadcast_to(self_sel, (SUBLANES, LANES))

    lane = _iota((H_MIX, LANES), 1)
    row = _iota((H_MIX, LANES), 0)
    qbd = jnp.where((lane // HD) == (row // G_Q), qn_ref[...], 0.0) * SCALE
    col = lambda b: tabt_ref[H_MIX:2 * H_MIX, b:b + 1]

    cmp = cmp_ref[...]
    ncp = cmp.shape[0]
    n_idx = _iota((1, ncp), 1)
    rel_c = t - (n_idx * CMP_STRIDE + (CMP_LEN - 1))
    ok = jnp.broadcast_to((rel_c >= 0) & (n_idx < n_cmp), (H_MIX, ncp))
    s = jnp.where(ok, _dot_nt(qbd.astype(BF16), cmp[:, 0:LANES].astype(BF16)) + _bias_from_table(rel_c, col), NEG)
    mx = jnp.max(s, axis=1, keepdims=True)
    p = jnp.where(ok, jnp.exp(s - mx), 0.0)
    den = jnp.sum(p, axis=1, keepdims=True)
    pc = p / jnp.where(den == 0.0, 1.0, den)
    oc_ref[...] = _dot(pc.astype(BF16), cmp[:, LANES:2 * LANES].astype(BF16))

    nbp = -(-n_sel // LANES) * LANES
    rown = _iota((H_MIX, ncp), 0)
    pcs = jnp.where(rown == 0, jnp.sum(pc[0:G_Q], axis=0, keepdims=True),
                    jnp.where(rown == 1, jnp.sum(pc[G_Q:2 * G_Q], axis=0, keepdims=True), 0.0))
    spread = jnp.where((_iota((ncp, nbp), 0) >= 4 * _iota((ncp, nbp), 1) - 1)
                       & (_iota((ncp, nbp), 0) <= 4 * _iota((ncp, nbp), 1) + 3), 1.0, 0.0).astype(BF16)
    imp = _split_dot(pcs, spread, 3)
    blk = _iota((H_MIX, nbp), 1)
    tb = t // SEL_LEN
    forced = (blk == 0) | (blk == tb) | (blk == tb - 1)
    score = jnp.where(forced, BIG, jnp.where(blk <= tb, imp, NEG))
    score = jnp.where(blk < n_sel, score, -3e38)
    selb = _rowwise_topk_mask(score, n_top)
    pg = _iota((n_pages, nbp), 0)
    bl = _iota((n_pages, nbp), 1)
    e2 = jnp.where((_iota((nbp, PAGE), 0) % 2) == (_iota((nbp, PAGE), 1) // SEL_LEN), 1.0, 0.0).astype(BF16)
    for g in range(KV_G):
        pm = jnp.where((bl // 2) == pg, selb[g:g + 1, :], 0.0).astype(BF16)
        nsel_ref[g] = _dot(pm, e2)

    win = win_ref[...]
    ws = win.shape[0]
    rel_w = ws - _iota((1, ws), 1)
    s = _dot_nt(qbd.astype(BF16), win[:, 0:LANES].astype(BF16)) + _bias_from_table(rel_w, col)
    winn = winn_ref[...]
    s_self = jnp.sum(qbd * winn[:, 0:LANES], axis=1, keepdims=True) + col(0)
    mx = jnp.maximum(jnp.max(s, axis=1, keepdims=True), s_self)
    p = jnp.exp(s - mx)
    ps = jnp.exp(s_self - mx)
    den = jnp.sum(p, axis=1, keepdims=True) + ps
    ow_ref[...] = (_dot(p.astype(BF16), win[:, LANES:2 * LANES].astype(BF16)) + ps * winn[:, LANES:2 * LANES]) / den


def _s_select_call(layer, tabt, isc, iq128, iw, kixn, qn, comp, state_win, winn, past):
    bs, n_pages, _ = isc.shape
    length = past + 1
    k_top = min(TOPK_DSA, length // 4)
    n_cmp = (length - CMP_LEN) // CMP_STRIDE + 1
    n_sel = -(-length // SEL_LEN)
    n_top = min(SEL_TOP, n_sel)
    ws = state_win.shape[2]
    b3 = lambda shape: pl.BlockSpec((None,) + shape, lambda b: (b, 0, 0))
    return pl.pallas_call(
        functools.partial(_s_select_kernel, past=past, k_top=k_top, n_cmp=n_cmp, n_sel=n_sel, n_top=n_top),
        grid=(bs,),
        in_specs=[pl.BlockSpec((2 * H_MIX, N_BUCKETS), lambda b: (0, 0)),
                  b3((n_pages, PAGE)), b3((H_MIX, LANES)), b3((H_MIX, 1)), b3((1, LANES)), b3((H_MIX, LANES)),
                  b3((comp.shape[1], 2 * LANES)),
                  pl.BlockSpec((None, None, ws, C_WIN), lambda b: (layer, b, 0, 0)),
                  b3((1, C_WIN))],
        out_specs=[b3((n_pages, PAGE)),
                   pl.BlockSpec((None, KV_G, n_pages, PAGE), lambda b: (b, 0, 0, 0)),
                   b3((SUBLANES, LANES)), b3((H_MIX, LANES)), b3((H_MIX, LANES))],
        out_shape=[jax.ShapeDtypeStruct((bs, n_pages, PAGE), F32),
                   jax.ShapeDtypeStruct((bs, KV_G, n_pages, PAGE), F32),
                   jax.ShapeDtypeStruct((bs, SUBLANES, LANES), F32),
                   jax.ShapeDtypeStruct((bs, H_MIX, LANES), F32),
                   jax.ShapeDtypeStruct((bs, H_MIX, LANES), F32)],
        compiler_params=pltpu.CompilerParams(dimension_semantics=("arbitrary",), vmem_limit_bytes=VMEM_LIMIT),
        name="sample_select",
    )(tabt, isc, iq128, iw, kixn, qn, comp, state_win, winn)


def _s_sparse_kernel(pt_ref, tabt_ref, qd_ref, qn_ref, dkn_ref, dvn_ref, nkn_ref, nvn_ref, dsel_ref, nsel_ref,
                     misc_ref, oc_ref, ow_ref, gate_ref, dkv_ref, n20_ref, n21_ref, n22_ref, ods_ref, ons_ref,
                     m_ref, l_ref, acc_ref, *, n_pages, past):
    j = pl.program_id(1)
    lane = _iota((H_MIX, LANES), 1)
    row = _iota((H_MIX, LANES), 0)
    grp = (lane // HD) == (row // G_Q)
    qd = jnp.where(grp, qd_ref[...], 0.0) * SCALE
    qn = jnp.where(grp, qn_ref[...], 0.0) * SCALE
    col_d = lambda b: tabt_ref[0:H_MIX, b:b + 1]
    col_n = lambda b: tabt_ref[H_MIX:2 * H_MIX, b:b + 1]

    @pl.when(j == 0)
    def _():
        f = misc_ref[0:1, 0:1]
        m_ref[0] = jnp.where(f > 0.5, jnp.sum(qd * dkn_ref[...], axis=1, keepdims=True) + col_d(0), NEG)
        l_ref[0] = jnp.broadcast_to(f, (H_MIX, 1))
        acc_ref[0] = f * jnp.broadcast_to(dvn_ref[...], (H_MIX, LANES))
        m_ref[1] = jnp.sum(qn * nkn_ref[...], axis=1, keepdims=True) + col_n(0)
        l_ref[1] = jnp.ones((H_MIX, 1), F32)
        acc_ref[1] = jnp.broadcast_to(nvn_ref[...], (H_MIX, LANES))

    rel = past - (j * PAGE + _iota((1, PAGE), 1))
    dkv = dkv_ref[...]
    s = _dot_nt(qd.astype(BF16), dkv[:, 0:LANES].astype(BF16)) + _bias_from_table(rel, col_d)
    maskb = jnp.broadcast_to(dsel_ref[...], (H_MIX, PAGE)) > 0.5
    _flash_update(s, maskb, dkv[:, LANES:2 * LANES].astype(BF16), m_ref.at[0], l_ref.at[0], acc_ref.at[0])

    lane_p = _iota((PAGE, LANES), 1)
    n21r = pltpu.roll(n21_ref[...], HD, 1)
    ks = jnp.where(lane_p < HD, pltpu.roll(n20_ref[...], HD, 1), n21r)
    vs = jnp.where(lane_p < HD, n21r, pltpu.roll(n22_ref[...], HD, 1))
    s = _dot_nt(qn.astype(BF16), ks.astype(BF16)) + _bias_from_table(rel, col_n)
    nsel = nsel_ref[...]
    maskb = jnp.where(row[:, 0:PAGE] < G_Q, nsel[0], nsel[1]) > 0.5
    _flash_update(s, maskb, vs.astype(BF16), m_ref.at[1], l_ref.at[1], acc_ref.at[1])

    @pl.when(j == n_pages - 1)
    def _():
        def pick(o):
            return jnp.where(row < G_Q, o, pltpu.roll(o, HD, 1))[:, 0:HD]
        ods_ref[...] = pick(acc_ref[0] / l_ref[0])
        o_s = pick(acc_ref[1] / l_ref[1])
        ons_ref[...] = (_sigmoid(gate_ref[0]) * pick(oc_ref[...]) + _sigmoid(gate_ref[1]) * o_s
                        + _sigmoid(gate_ref[2]) * pick(ow_ref[...]))


def _s_sparse_call(layer, page_table, tabt, qd, qn, dkn, dvn, nkn, nvn, dsel, nsel, misc, oc, ow, gates, cache_kv, past):
    bs, n_pages = page_table.shape
    b3 = lambda shape: pl.BlockSpec((None,) + shape, lambda b, j, pt: (b, 0, 0))
    page = lambda width, idx: pl.BlockSpec((None, None, PAGE, width), lambda b, j, pt: (layer, pt[b, j], 0, idx))
    grid_spec = pltpu.PrefetchScalarGridSpec(
        num_scalar_prefetch=1, grid=(bs, n_pages),
        in_specs=[pl.BlockSpec((2 * H_MIX, N_BUCKETS), lambda b, j, pt: (0, 0)),
                  b3((H_MIX, LANES)), b3((H_MIX, LANES)),
                  b3((1, LANES)), b3((1, LANES)), b3((1, LANES)), b3((1, LANES)),
                  pl.BlockSpec((None, None, 1, PAGE), lambda b, j, pt: (b, j, 0, 0)),
                  pl.BlockSpec((None, KV_G, None, 1, PAGE), lambda b, j, pt: (b, 0, j, 0, 0)),
                  b3((SUBLANES, LANES)), b3((H_MIX, LANES)), b3((H_MIX, LANES)),
                  pl.BlockSpec((None, 3, H_MIX, 1), lambda b, j, pt: (b, 0, 0, 0)),
                  page(2 * LANES, 8), page(LANES, 20), page(LANES, 21), page(LANES, 22)],
        out_specs=[b3((H_MIX, HD)), b3((H_MIX, HD))],
        scratch_shapes=[pltpu.VMEM((2, H_MIX, 1), F32), pltpu.VMEM((2, H_MIX, 1), F32), pltpu.VMEM((2, H_MIX, LANES), F32)])
    return pl.pallas_call(
        functools.partial(_s_sparse_kernel, n_pages=n_pages, past=past),
        grid_spec=grid_spec,
        out_shape=[jax.ShapeDtypeStruct((bs, H_MIX, HD), F32)] * 2,
        compiler_params=pltpu.CompilerParams(dimension_semantics=("arbitrary",) * 2, vmem_limit_bytes=VMEM_LIMIT),
        name="sample_sparse_attention",
    )(page_table, tabt, qd, qn, dkn, dvn, nkn, nvn, dsel.reshape(bs, n_pages, 1, PAGE),
      nsel.reshape(bs, KV_G, n_pages, 1, PAGE), misc, oc, ow, gates, cache_kv, cache_kv, cache_kv, cache_kv)


_O_LOGF = C_KV
_O_WIN = C_KV + H_MIX
_O_Q = _O_WIN + C_WIN
_O_IDXW = _O_Q + 4 * BRANCH_W
_O_NSAQ = _O_IDXW + H_MIX
_O_NSAG = _O_NSAQ + BRANCH_W
_O_BG = _O_NSAG + 3 * H_MIX
_SM_IDXW = SUBLANES
_SM_NSAG = 2 * SUBLANES


def _cmp_weights(w1, w2, pos):
    def blockdiag(w):
        z = jnp.zeros_like(w)
        return jnp.concatenate([jnp.concatenate([w, z], axis=-1), jnp.concatenate([z, w], axis=-1)], axis=-2)
    w1b = blockdiag(w1.reshape(2, CMP_LEN, HD, HD)).astype(BF16)
    w2b = blockdiag(w2).astype(BF16)
    posb = jnp.concatenate([pos, pos], axis=1)
    return w1b, posb, w2b


def _layer_weights(i, w_in, b_forget, w_cmp1, w_cmp2, cmp_pos, w_branch, w_out, ln1_g, ln1_b, w_gate, w_up, w_down,
                   ln2_g, ln2_b, w_pg, w_pe):
    w = w_in[i]
    d = w.shape[0]
    pad = jnp.zeros((d, LANES - _SM_NSAG - 3 * H_MIX), F32)
    w_small = jnp.concatenate([w[:, _O_LOGF:_O_LOGF + H_MIX], w[:, _O_IDXW:_O_IDXW + H_MIX],
                               w[:, _O_NSAG:_O_NSAG + 3 * H_MIX], pad], axis=1)
    b_small = jnp.concatenate([b_forget[i], jnp.zeros((LANES - H_MIX,), F32)]).reshape(1, LANES)
    w_q = jnp.concatenate([w[:, _O_Q:_O_IDXW], w[:, _O_NSAQ:_O_NSAG]], axis=1)
    w1b, posb, w2b = _cmp_weights(w_cmp1[i], w_cmp2[i], cmp_pos[i])
    return dict(
        w_kv=w[:, :C_KV].astype(BF16), w_small=w_small.astype(BF16), b_small=b_small,
        w_win=w[:, _O_WIN:_O_Q].astype(BF16), w_q=w_q.astype(BF16),
        w_bg=[w[:, _O_BG + n * D_MODEL:_O_BG + (n + 1) * D_MODEL].astype(BF16) for n in range(4)],
        w_br=[w_branch[i, n].astype(BF16) for n in range(4)],
        w_out=w_out[i].astype(BF16), ln1_g=ln1_g[i].reshape(1, -1), ln1_b=ln1_b[i].reshape(1, -1),
        w_gate=w_gate[i].astype(BF16), w_up=w_up[i].astype(BF16), w_down=w_down[i].astype(BF16),
        ln2_g=ln2_g[i].reshape(1, -1), ln2_b=ln2_b[i].reshape(1, -1),
        w_pg=w_pg[i].astype(BF16), w_pe=w_pe[i].astype(BF16), w1b=w1b, posb=posb, w2b=w2b)


def _project(x, lw, tag):
    tm = _row_tile(x.shape[0])
    rows, = _dense([x], [(0, lw['w_kv'])], [], _ep_identity, [F32], tm=tm, tn=C_KV, tk=512, name=tag + "_proj_kv")
    small, lsig = _dense([x], [(0, lw['w_small'])], [('row', lw['b_small'])], _ep_small, [F32, F32],
                         tm=tm, tn=LANES, name=tag + "_proj_small")
    win, = _dense([x], [(0, lw['w_win'])], [], _ep_identity, [F32], tm=tm, tn=C_WIN, name=tag + "_proj_win")
    qall, = _dense([x], [(0, lw['w_q'])], [], _ep_identity, [F32], tm=tm, tn=BRANCH_W, name=tag + "_proj_q")
    return rows, small, lsig, win, qall


def _tail(x, br, p_emb, lw, tag):
    m = x.shape[0]
    tm = _row_tile(m)
    tm_merge = 256 if m % 256 == 0 else m
    pairs = [(0, lw['w_bg'][n]) for n in range(4)] + [(1 + n, lw['w_br'][n]) for n in range(4)]
    mixpre, = _dense([x] + br, pairs, [], _ep_merge, [F32], tm=tm_merge, tn=512, name=tag + "_merge")
    x1, = _dense([mixpre], [(0, lw['w_out'])], [('tile', x), ('row', lw['ln1_g']), ('row', lw['ln1_b'])],
                 _ep_ln_residual, [F32], tm=tm, tn=D_MODEL, tk=512, name=tag + "_out_ln")
    h, = _dense([x1], [(0, lw['w_gate']), (0, lw['w_up'])], [], _ep_swiglu, [BF16], tm=tm, tn=512, name=tag + "_swiglu")
    x2, = _dense([h], [(0, lw['w_down'])], [('tile', x1), ('row', lw['ln2_g']), ('row', lw['ln2_b'])],
                 _ep_ln_residual, [F32], tm=tm, tn=D_MODEL, tk=_k_tile(h.shape[1]), name=tag + "_down_ln")
    x3, = _dense([x2, p_emb], [(0, lw['w_pg']), (1, lw['w_pe'])], [('tile', x2)], _ep_ple_gate, [F32],
                 tm=tm, tn=512, name=tag + "_ple")
    return x3


def _prompt_layer(x, p_emb, lw, tab, b, t):
    rows, small, lsig, win, qall = _project(x, lw, "prompt")
    m = x.shape[0]
    aux = jnp.concatenate([rows[:, 2368:C_KV], rows[:, 2304:2368], jnp.zeros((m, HD), F32)], axis=1)
    cum, cumt = _cumsum_call(lsig, b, t)
    br_a = _sb_call(qall, rows, b, t)
    br_b = _fox_call(qall, rows, cum, cumt, b, t)
    br_c = _dsa_call(tab, qall, small, aux, rows, b, t)
    comp = _compress_call(aux, lw['w1b'], lw['posb'], lw['w2b'], b, t)
    br_d = _nsa_call(tab, qall, small, comp, aux, win, b, t)
    x3 = _tail(x, [br_a, br_b, br_c, br_d], p_emb, lw, "prompt")
    return x3, rows, lsig[:, :H_MIX], win


def _sample_layer(layer, x, p_emb, lw, tabt, page_table, cache_kv, cache_lft, state_win, past):
    bs = x.shape[0]
    rows, small, lsig, win, qall = _project(x, lw, "sample")
    r3 = lambda a: a.reshape(bs, 1, a.shape[-1])
    heads = lambda a: a.reshape(bs, H_MIX, HD)
    twice = lambda a: jnp.concatenate([heads(a), heads(a)], axis=2)
    q_sb, q_fx, q_ds, q_ix, q_ns = [qall[:, BRANCH_W * c:BRANCH_W * (c + 1)] for c in range(5)]
    o_sb, o_fx = _s_sbfox_call(layer, page_table, r3(q_sb), r3(q_fx), r3(rows[:, 1024:1536]), r3(rows[:, 1536:2048]),
                               lsig[:, :H_MIX].reshape(bs, H_MIX, 1), cache_kv, cache_lft)
    iq384 = jnp.concatenate([heads(q_ix), jnp.zeros((bs, H_MIX, 3 * LANES - HD), F32)], axis=2)
    iw = small[:, _SM_IDXW:_SM_IDXW + H_MIX].reshape(bs, H_MIX, 1)
    isc, comp = _s_index_call(layer, page_table, iq384, iw, cache_kv, lw['w1b'], lw['posb'], lw['w2b'])
    kixn = jnp.concatenate([rows[:, 2304:2368], jnp.zeros((bs, HD), F32)], axis=1)
    n_pages = page_table.shape[1]
    dsel, nsel, misc, oc, ow = _s_select_call(layer, tabt, isc.reshape(bs, n_pages, PAGE), iq384[:, :, :LANES], iw,
                                              r3(kixn), twice(q_ns), comp, state_win, r3(win), past)
    gates = small[:, _SM_NSAG:_SM_NSAG + 3 * H_MIX].reshape(bs, 3, H_MIX, 1)
    o_ds, o_ns = _s_sparse_call(layer, page_table, tabt, twice(q_ds), twice(q_ns),
                                r3(rows[:, 2048:2176]), r3(rows[:, 2176:2304]), r3(rows[:, 2624:2752]), r3(rows[:, 2752:C_KV]),
                                dsel, nsel, misc, oc, ow, gates, cache_kv, past)
    br = [o_sb.reshape(bs, BRANCH_W), o_fx.reshape(bs, BRANCH_W), o_ds.reshape(bs, BRANCH_W), o_ns.reshape(bs, BRANCH_W)]
    x3 = _tail(x, br, p_emb, lw, "sample")
    return x3, rows, lsig[:, :H_MIX], win


def kernel(x_prompt, x_sample, cache_kv, cache_logf, state_win, page_table, p_prompt, p_sample, w_in, b_forget, w_cmp1, w_cmp2, cmp_pos, w_branch, w_out, ln1_g, ln1_b, w_gate, w_up, w_down, ln2_g, ln2_b, w_pg, w_pe, rpb_table):
    depth = w_in.shape[0]
    b, t, d = x_prompt.shape
    bs = x_sample.shape[0]
    n_pages = page_table.shape[1]
    past = n_pages * PAGE
    win_p = min(WINDOW, t)
    win_s = state_win.shape[2]
    assert d == D_MODEL and x_sample.shape[1] == 1 and t % QBLK == 0 and cache_kv.shape[2] == PAGE
    cache_lft = jnp.swapaxes(cache_logf, 2, 3)
    tabt = rpb_table.T
    yp = x_prompt.reshape(b * t, d)
    ys = x_sample.reshape(bs, d)
    kv_p, kv_s, lf_p, lf_s, wn_p, wn_s = [], [], [], [], [], []
    for i in range(depth):
        lw = _layer_weights(i, w_in, b_forget, w_cmp1, w_cmp2, cmp_pos, w_branch, w_out, ln1_g, ln1_b,
                            w_gate, w_up, w_down, ln2_g, ln2_b, w_pg, w_pe)
        yp, rows, lf, wrow = _prompt_layer(yp, p_prompt[i].reshape(b * t, -1), lw, rpb_table, b, t)
        kv_p.append(rows.reshape(b, t, C_KV))
        lf_p.append(lf.reshape(b, t, H_MIX))
        wn_p.append(wrow.reshape(b, t, C_WIN)[:, t - win_p:])
        ys, rows, lf, wrow = _sample_layer(i, ys, p_sample[i].reshape(bs, -1), lw, tabt, page_table, cache_kv,
                                           cache_lft, state_win, past)
        kv_s.append(rows.reshape(bs, 1, C_KV))
        lf_s.append(lf.reshape(bs, 1, H_MIX))
        wn_s.append(jnp.concatenate([state_win[i], wrow.reshape(bs, 1, C_WIN)], axis=1)[:, -win_s:])
    return (yp.reshape(b, t, d), ys.reshape(bs, 1, d), jnp.stack(kv_p), jnp.stack(kv_s), jnp.stack(lf_p),
            jnp.stack(lf_s), jnp.stack(wn_p), jnp.stack(wn_s))
```
